```python
import jax
import jax.numpy as jnp
from jax import lax
import numpy as np

D_MODEL = 4096
BATCH = 4
SEQ = 2048
DEPTH = 1
DEC_BATCH = 32
DEC_SEQ = 1
PAST_LEN = 8192
PAGE_SIZE = 128

HEAD_DIM = 128
WINDOWS = (128, 512, 2048)
DILATIONS = (1, 4, 16)
N_DIL_GROUPS = 3
HEADS_PER_GROUP = 4
N_ATTN_HEADS = N_DIL_GROUPS * HEADS_PER_GROUP
D_ATTN = N_ATTN_HEADS * HEAD_DIM
ROT_DIM = HEAD_DIM // 4
ROPE_THETA = 500000.0
D_CONV = 3 * D_MODEL // 8
CONV_WIDTH = 3
MEM_LEN = 256
MEM_HEADS = 4
MEM_HEAD_DIM = D_MODEL // 16
D_MEM = MEM_HEADS * MEM_HEAD_DIM
N_BRANCH = 3
N_EXPERT_GROUPS = 8
EXPERTS_PER_GROUP = 8
N_EXPERTS = N_EXPERT_GROUPS * EXPERTS_PER_GROUP
TOP_K = 2
D_EXPERT = D_MODEL // 4
RMS_EPS = 1e-6
NEG_INF = -1e30

kernel_name = 'gated_conv_dilated_swa_memxattn_hmoe_step'


def rmsnorm(x, g):
    xf = x.astype(jnp.float32)
    y = xf * lax.rsqrt(jnp.mean(xf * xf, axis=-1, keepdims=True) + RMS_EPS)
    return (y * g.astype(jnp.float32)).astype(x.dtype)


def rope(x, pos):
    half = ROT_DIM // 2
    inv_freq = ROPE_THETA ** (-jnp.arange(half, dtype=jnp.float32) / half)
    ang = pos.astype(jnp.float32)[:, None] * inv_freq[None, :]
    cos = jnp.cos(ang)[None, :, None, :].astype(x.dtype)
    sin = jnp.sin(ang)[None, :, None, :].astype(x.dtype)
    x1 = x[..., :half]
    x2 = x[..., half:ROT_DIM]
    return jnp.concatenate([x1 * cos - x2 * sin, x2 * cos + x1 * sin, x[..., ROT_DIM:]], axis=-1)


def project(h, w_in, b_gate):
    z = h @ w_in
    sizes = (D_CONV, D_CONV, D_CONV, D_ATTN, D_ATTN, D_ATTN, D_MEM)
    pieces = []
    off = 0
    for s in sizes:
        pieces.append(z[..., off:off + s])
        off += s
    gates = jax.nn.sigmoid((z[..., off:] + b_gate).astype(jnp.float32)).astype(h.dtype)
    return pieces[0], pieces[1], pieces[2], pieces[3], pieces[4], pieces[5], pieces[6], gates


def dilated_prompt(q, k, v, window, dil):
    B, S, H, Dh = q.shape
    nwin = window // dil
    seg = dil * nwin
    nb = -(-S // seg)
    Sp = nb * seg

    def to_blocks(t):
        t = jnp.pad(t, ((0, 0), (0, Sp - S), (0, 0), (0, 0)))
        t = t.reshape(B, nb * nwin, dil, H, Dh).transpose(0, 2, 1, 3, 4)
        return t.reshape(B, dil, nb, nwin, H, Dh)

    def with_prev(t):
        prev = jnp.pad(t[:, :, :-1], ((0, 0), (0, 0), (1, 0), (0, 0), (0, 0), (0, 0)))
        return jnp.concatenate([prev, t], axis=3)

    qb = to_blocks(q)
    kw = with_prev(to_blocks(k))
    vw = with_prev(to_blocks(v))
    s = jnp.einsum('brnqhd,brnkhd->brnhqk', qb, kw, preferred_element_type=jnp.float32) * (Dh ** -0.5)
    i = jnp.arange(nwin)[:, None]
    c = jnp.arange(2 * nwin)[None, :]
    dist = i + nwin - c
    band = (dist >= 0) & (dist <= nwin)
    not_first = jnp.arange(nb)[:, None, None] > 0
    mask = band[None] & (not_first | (c >= nwin)[None])
    s = jnp.where(mask[None, None, :, None], s, NEG_INF)
    m = jnp.max(s, axis=-1, keepdims=True)
    p = jnp.exp(s - m)
    den = jnp.sum(p, axis=-1, keepdims=True)
    o = jnp.einsum('brnhqk,brnkhd->brnhqd', p, vw, preferred_element_type=jnp.float32) / den
    lse = (m + jnp.log(den))[..., 0]
    o = o.transpose(0, 2, 4, 1, 3, 5).reshape(B, Sp, H, Dh)[:, :S]
    lse = lse.transpose(0, 2, 4, 1, 3).reshape(B, Sp, H)[:, :S]
    keep = min(window, S)
    return o, lse, (k[:, S - keep:], v[:, S - keep:])


def dilated_sample(q, k, v, k_buf, v_buf, window, dil):
    L = k_buf.shape[1]
    T = q.shape[1]
    nwin = window // dil
    Dh = q.shape[-1]
    kc = jnp.concatenate([k_buf.astype(k.dtype), k], axis=1)
    vc = jnp.concatenate([v_buf.astype(v.dtype), v], axis=1)
    idx = L + jnp.arange(T)[:, None] - dil * jnp.arange(nwin + 1)[None, :]
    valid = idx >= 0
    idx = jnp.maximum(idx, 0)
    kg = kc[:, idx]
    vg = vc[:, idx]
    s = jnp.einsum('bthd,btkhd->bthk', q, kg, preferred_element_type=jnp.float32) * (Dh ** -0.5)
    s = jnp.where(valid[None, :, None, :], s, NEG_INF)
    m = jnp.max(s, axis=-1, keepdims=True)
    p = jnp.exp(s - m)
    den = jnp.sum(p, axis=-1, keepdims=True)
    o = jnp.einsum('bthk,btkhd->bthd', p, vg, preferred_element_type=jnp.float32) / den
    lse = (m + jnp.log(den))[..., 0]
    return o, lse, (kc[:, T:], vc[:, T:])


def prompt_group_attend(g, q, k, v):
    return dilated_prompt(q, k, v, WINDOWS[g], DILATIONS[g])


def make_sample_group_attend(k_bufs, v_bufs):
    def attend(g, q, k, v):
        return dilated_sample(q, k, v, k_bufs[g], v_bufs[g], WINDOWS[g], DILATIONS[g])
    return attend


def memory_kv(mem, mem_norm, w_mem_kv, mk_norm):
    B, M, _ = mem.shape
    kv = rmsnorm(mem, mem_norm) @ w_mem_kv
    k = rmsnorm(kv[..., :D_MEM].reshape(B, M, MEM_HEADS, MEM_HEAD_DIM), mk_norm)
    v = kv[..., D_MEM:].reshape(B, M, MEM_HEADS, MEM_HEAD_DIM)
    return k, v


def memory_attend(mq, mem_k, mem_v, mq_norm):
    B, T, _ = mq.shape
    q = rmsnorm(mq.reshape(B, T, MEM_HEADS, MEM_HEAD_DIM), mq_norm)
    s = jnp.einsum('bthd,bmhd->bhtm', q, mem_k.astype(q.dtype), preferred_element_type=jnp.float32) * (MEM_HEAD_DIM ** -0.5)
    p = jax.nn.softmax(s, axis=-1)
    o = jnp.einsum('bhtm,bmhd->bthd', p.astype(mem_v.dtype), mem_v)
    return o.reshape(B, T, D_MEM).astype(mq.dtype)


def hier_moe(h, w_rg, b_rg, w_re, b_re, w_gate, w_up, w_down):
    lead = h.shape[:-1]
    hf = h.reshape(-1, D_MODEL)
    n_tok = hf.shape[0]
    g_logits = (hf @ w_rg).astype(jnp.float32) + b_rg.astype(jnp.float32)
    g_idx = jnp.argmax(g_logits, axis=-1)
    g_gate = jnp.max(jax.nn.softmax(g_logits, axis=-1), axis=-1, keepdims=True)
    e_logits = ((hf @ w_re).astype(jnp.float32) + b_re.astype(jnp.float32)).reshape(n_tok, N_EXPERT_GROUPS, EXPERTS_PER_GROUP)
    e_logits = e_logits[jnp.arange(n_tok), g_idx]
    top_v, top_i = lax.top_k(e_logits, TOP_K)
    gate = jax.nn.softmax(top_v, axis=-1) * g_gate
    expert = g_idx[:, None] * EXPERTS_PER_GROUP + top_i
    n_asg = n_tok * TOP_K
    blk = 128 if n_asg >= 128 * N_EXPERTS else 8
    n_blk = -(-(n_asg + N_EXPERTS * (blk - 1)) // blk)
    e_flat = expert.reshape(-1)
    w_flat = gate.reshape(-1)
    tok = jnp.arange(n_asg) // TOP_K
    order = jnp.argsort(e_flat)
    e_s = e_flat[order]
    tok_s = tok[order]
    w_s = w_flat[order]
    counts = jnp.bincount(e_flat, length=N_EXPERTS)
    padded = ((counts + blk - 1) // blk) * blk
    start = jnp.cumsum(counts) - counts
    pend = jnp.cumsum(padded)
    pstart = pend - padded
    dest = pstart[e_s] + jnp.arange(n_asg) - start[e_s]
    xbuf = jnp.zeros((n_blk * blk, D_MODEL), h.dtype).at[dest].set(hf[tok_s])
    block_e = jnp.clip(jnp.searchsorted(pend, jnp.arange(n_blk) * blk, side='right'), 0, N_EXPERTS - 1)

    def expert_block(args):
        xb, e = args
        a = jax.nn.silu(xb @ w_gate[e]) * (xb @ w_up[e])
        return a @ w_down[e]

    ybuf = lax.map(expert_block, (xbuf.reshape(n_blk, blk, D_MODEL), block_e)).reshape(n_blk * blk, D_MODEL)
    y = jnp.zeros((n_tok, D_MODEL), jnp.float32).at[tok_s].add(ybuf[dest].astype(jnp.float32) * w_s[:, None])
    return y.astype(h.dtype).reshape(*lead, D_MODEL)


def layer(x, pos, conv_prev, group_attend, mem_k, mem_v, norm1, w_in, b_gate, conv_w, q_norm, k_norm,
          mq_norm, w_conv_out, w_attn_out, w_mem_out, w_o, norm2, w_router_group, b_router_group,
          w_router_expert, b_router_expert, w_exp_gate, w_exp_up, w_exp_down):
    Bn, T, _ = x.shape
    h = rmsnorm(x, norm1)
    cx, cb, cc, q, k, v, mq, gates = project(h, w_in, b_gate)
    u = cc * cx
    uc = jnp.concatenate([conv_prev.astype(u.dtype), u], axis=1)
    conv = uc[:, 0:T] * conv_w[0]
    for j in range(1, CONV_WIDTH):
        conv = conv + uc[:, j:j + T] * conv_w[j]
    y_conv = (cb * conv) @ w_conv_out
    new_conv = uc[:, T:]
    hs = (Bn, T, N_ATTN_HEADS, HEAD_DIM)
    q = rope(rmsnorm(q.reshape(hs), q_norm), pos)
    k = rope(rmsnorm(k.reshape(hs), k_norm), pos)
    v = v.reshape(hs)
    outs, lses, bufs = [], [], []
    for g in range(N_DIL_GROUPS):
        sl = slice(g * HEADS_PER_GROUP, (g + 1) * HEADS_PER_GROUP)
        o, lse, buf = group_attend(g, q[:, :, sl], k[:, :, sl], v[:, :, sl])
        outs.append(o)
        lses.append(lse)
        bufs.append(buf)
    wts = jax.nn.softmax(jnp.stack(lses, axis=0), axis=0)
    att = jnp.concatenate([outs[g] * wts[g][..., None] for g in range(N_DIL_GROUPS)], axis=2)
    y_att = att.reshape(Bn, T, D_ATTN).astype(x.dtype) @ w_attn_out
    y_mem = memory_attend(mq, mem_k, mem_v, mq_norm) @ w_mem_out
    mix = (gates[..., :D_MODEL] * y_conv + gates[..., D_MODEL:2 * D_MODEL] * y_att
           + gates[..., 2 * D_MODEL:] * y_mem)
    x = x + mix @ w_o
    x = x + hier_moe(rmsnorm(x, norm2), w_router_group, b_router_group, w_router_expert, b_router_expert,
                     w_exp_gate, w_exp_up, w_exp_down)
    return x, new_conv, bufs


def setup_inputs(seed: int = 0) -> dict:
    key = jax.random.key(seed)
    keys = iter(jax.random.split(key, 48))

    def normal(shape, scale=1.0):
        return jax.random.normal(next(keys), shape, jnp.float32) * scale

    def gain(n):
        return 1.0 + 0.02 * normal((DEPTH, n))

    lw = [min(w, PAST_LEN) for w in WINDOWS]
    n_in = 3 * D_CONV + 3 * D_ATTN + D_MEM + N_BRANCH * D_MODEL
    return {
        'x_prompt': normal((BATCH, SEQ, D_MODEL)),
        'x_sample': normal((DEC_BATCH, DEC_SEQ, D_MODEL)),
        'mem_prompt': normal((BATCH, MEM_LEN, D_MODEL)),
        'cache_conv': normal((DEPTH, DEC_BATCH, CONV_WIDTH - 1, D_CONV)),
        'cache_k_w128': normal((DEPTH, DEC_BATCH, lw[0], HEADS_PER_GROUP, HEAD_DIM)),
        'cache_v_w128': normal((DEPTH, DEC_BATCH, lw[0], HEADS_PER_GROUP, HEAD_DIM)),
        'cache_k_w512': normal((DEPTH, DEC_BATCH, lw[1], HEADS_PER_GROUP, HEAD_DIM)),
        'cache_v_w512': normal((DEPTH, DEC_BATCH, lw[1], HEADS_PER_GROUP, HEAD_DIM)),
        'cache_k_w2048': normal((DEPTH, DEC_BATCH, lw[2], HEADS_PER_GROUP, HEAD_DIM)),
        'cache_v_w2048': normal((DEPTH, DEC_BATCH, lw[2], HEADS_PER_GROUP, HEAD_DIM)),
        'cache_mem_k': normal((DEPTH, DEC_BATCH, MEM_LEN, MEM_HEADS, MEM_HEAD_DIM)),
        'cache_mem_v': normal((DEPTH, DEC_BATCH, MEM_LEN, MEM_HEADS, MEM_HEAD_DIM)),
        'norm1': gain(D_MODEL),
        'w_in': normal((DEPTH, D_MODEL, n_in), D_MODEL ** -0.5),
        'b_gate': normal((DEPTH, N_BRANCH * D_MODEL), 0.1),
        'conv_w': normal((DEPTH, CONV_WIDTH, D_CONV), CONV_WIDTH ** -0.5),
        'q_norm': gain(HEAD_DIM),
        'k_norm': gain(HEAD_DIM),
        'mem_norm': gain(D_MODEL),
        'w_mem_kv': normal((DEPTH, D_MODEL, 2 * D_MEM), D_MODEL ** -0.5),
        'mq_norm': gain(MEM_HEAD_DIM),
        'mk_norm': gain(MEM_HEAD_DIM),
        'w_conv_out': normal((DEPTH, D_CONV, D_MODEL), D_CONV ** -0.5),
        'w_attn_out': normal((DEPTH, D_ATTN, D_MODEL), D_ATTN ** -0.5),
        'w_mem_out': normal((DEPTH, D_MEM, D_MODEL), D_MEM ** -0.5),
        'w_o': normal((DEPTH, D_MODEL, D_MODEL), D_MODEL ** -0.5),
        'norm2': gain(D_MODEL),
        'w_router_group': normal((DEPTH, D_MODEL, N_EXPERT_GROUPS), D_MODEL ** -0.5),
        'b_router_group': normal((DEPTH, N_EXPERT_GROUPS), 0.01),
        'w_router_expert': normal((DEPTH, D_MODEL, N_EXPERTS), D_MODEL ** -0.5),
        'b_router_expert': normal((DEPTH, N_EXPERTS), 0.01),
        'w_exp_gate': normal((DEPTH, N_EXPERTS, D_MODEL, D_EXPERT), D_MODEL ** -0.5),
        'w_exp_up': normal((DEPTH, N_EXPERTS, D_MODEL, D_EXPERT), D_MODEL ** -0.5),
        'w_exp_down': normal((DEPTH, N_EXPERTS, D_EXPERT, D_MODEL), D_EXPERT ** -0.5),
    }


def reference(x_prompt, x_sample, mem_prompt, cache_conv, cache_k_w128, cache_v_w128, cache_k_w512,
              cache_v_w512, cache_k_w2048, cache_v_w2048, cache_mem_k, cache_mem_v, norm1, w_in, b_gate,
              conv_w, q_norm, k_norm, mem_norm, w_mem_kv, mq_norm, mk_norm, w_conv_out, w_attn_out,
              w_mem_out, w_o, norm2, w_router_group, b_router_group, w_router_expert, b_router_expert,
              w_exp_gate, w_exp_up, w_exp_down):
    pos_p = jnp.arange(x_prompt.shape[1])
    pos_s = PAST_LEN + jnp.arange(x_sample.shape[1])
    conv_zero = jnp.zeros((x_prompt.shape[0], CONV_WIDTH - 1, D_CONV), x_prompt.dtype)
    xp = x_prompt
    xs = x_sample
    conv_p, conv_s, memk_p, memv_p = [], [], [], []
    k_p = [[] for _ in range(N_DIL_GROUPS)]
    v_p = [[] for _ in range(N_DIL_GROUPS)]
    k_s = [[] for _ in range(N_DIL_GROUPS)]
    v_s = [[] for _ in range(N_DIL_GROUPS)]
    for l in range(DEPTH):
        w = (norm1[l], w_in[l], b_gate[l], conv_w[l], q_norm[l], k_norm[l], mq_norm[l], w_conv_out[l],
             w_attn_out[l], w_mem_out[l], w_o[l], norm2[l], w_router_group[l], b_router_group[l],
             w_router_expert[l], b_router_expert[l], w_exp_gate[l], w_exp_up[l], w_exp_down[l])
        mk, mv = memory_kv(mem_prompt, mem_norm[l], w_mem_kv[l], mk_norm[l])
        xp, cp, bp = layer(xp, pos_p, conv_zero, prompt_group_attend, mk, mv, *w)
        s_attend = make_sample_group_attend((cache_k_w128[l], cache_k_w512[l], cache_k_w2048[l]),
                                            (cache_v_w128[l], cache_v_w512[l], cache_v_w2048[l]))
        xs, cs, bs = layer(xs, pos_s, cache_conv[l], s_attend, cache_mem_k[l], cache_mem_v[l], *w)
        conv_p.append(cp)
        conv_s.append(cs)
        memk_p.append(mk)
        memv_p.append(mv)
        for g in range(N_DIL_GROUPS):
            k_p[g].append(bp[g][0])
            v_p[g].append(bp[g][1])
            k_s[g].append(bs[g][0])
            v_s[g].append(bs[g][1])
    return (xp, xs, jnp.stack(conv_p), jnp.stack(conv_s),
            jnp.stack(k_p[0]), jnp.stack(v_p[0]), jnp.stack(k_p[1]), jnp.stack(v_p[1]),
            jnp.stack(k_p[2]), jnp.stack(v_p[2]),
            jnp.stack(k_s[0]), jnp.stack(v_s[0]), jnp.stack(k_s[1]), jnp.stack(v_s[1]),
            jnp.stack(k_s[2]), jnp.stack(v_s[2]),
            jnp.stack(memk_p), jnp.stack(memv_p))
```

```python
import functools

import jax
import jax.numpy as jnp
from jax import lax
from jax.experimental import pallas as pl
from jax.experimental.pallas import tpu as pltpu

F32 = jnp.float32
BF16 = jnp.bfloat16
HI = lax.Precision.HIGHEST

D_MODEL = 4096
SEQ = 2048
PAST_LEN = 8192
HEAD_DIM = 128
WINDOWS = (128, 512, 2048)
DILATIONS = (1, 4, 16)
HEADS_PER_GROUP = 4
D_GROUP = HEADS_PER_GROUP * HEAD_DIM
D_ATTN = 3 * D_GROUP
ROT_DIM = HEAD_DIM // 4
ROPE_THETA = 500000.0
D_CONV = 1536
MEM_LEN = 256
MEM_HEADS = 4
MEM_HEAD_DIM = 256
D_MEM = MEM_HEADS * MEM_HEAD_DIM
N_EXPERT_GROUPS = 8
EXPERTS_PER_GROUP = 8
N_EXPERTS = 64
TOP_K = 2
D_EXPERT = 1024
RMS_EPS = 1e-6
NEG_INF = -1e30
MOE_BLK = 128
N_ROUTER_PAD = 128

OFF_CX, OFF_CB, OFF_CC = 0, D_CONV, 2 * D_CONV
OFF_Q = 3 * D_CONV
OFF_K = OFF_Q + D_ATTN
OFF_V = OFF_K + D_ATTN
OFF_MQ = OFF_V + D_ATTN
OFF_G = OFF_MQ + D_MEM

MIB = 1024 * 1024


def _params(sem, vmem_mib):
    return pltpu.CompilerParams(dimension_semantics=sem, vmem_limit_bytes=vmem_mib * MIB)


def _dot(a, b):
    return jnp.dot(a, b, preferred_element_type=F32)


def _dot_hi(a, b):
    return jnp.dot(a, b, preferred_element_type=F32, precision=HI)


def _dot_nt(a, b):
    return lax.dot_general(a, b, (((1,), (1,)), ((), ())), preferred_element_type=F32)


def _rms(x, gain):
    ms = jnp.mean(x * x, axis=-1, keepdims=True)
    return x * lax.rsqrt(ms + RMS_EPS) * gain


def _rmsnorm_kernel(x_ref, g_ref, o_ref):
    o_ref[...] = _rms(x_ref[...].astype(F32), g_ref[...]).astype(o_ref.dtype)


def rmsnorm_rows(x, gain, out_dtype, tm):
    m, d = x.shape
    return pl.pallas_call(
        _rmsnorm_kernel,
        out_shape=jax.ShapeDtypeStruct((m, d), out_dtype),
        grid=(m // tm,),
        in_specs=[pl.BlockSpec((tm, d), lambda i: (i, 0)),
                  pl.BlockSpec((1, d), lambda i: (0, 0))],
        out_specs=pl.BlockSpec((tm, d), lambda i: (i, 0)),
        compiler_params=_params(("arbitrary",), 40),
        name="rmsnorm_rows",
    )(x, gain.reshape(1, d))


def _conv_proj_kernel(h_ref, hs_ref, wx_ref, wb_ref, wc_ref, cw_ref, p0_ref, p1_ref,
                      a_ref, nc_ref, as_ref, nc0_ref, nc1_ref, carry_ref, *, tm, tiles_per_seq):
    m = pl.program_id(1)
    h = h_ref[...]
    zx = _dot(h, wx_ref[...].astype(BF16))
    zb = _dot(h, wb_ref[...].astype(BF16))
    zc = _dot(h, wc_ref[...].astype(BF16))
    u = zc * zx

    @pl.when(m % tiles_per_seq == 0)
    def _():
        carry_ref[...] = jnp.zeros_like(carry_ref)

    c0 = carry_ref[0:1, :]
    c1 = carry_ref[1:2, :]
    row = lax.broadcasted_iota(jnp.int32, u.shape, 0)
    um1 = jnp.where(row == 0, c1, pltpu.roll(u, 1, 0))
    um2 = jnp.where(row == 0, c0, jnp.where(row == 1, c1, pltpu.roll(u, 2, 0)))
    cw = cw_ref[...]
    conv = um2 * cw[0:1, :] + um1 * cw[1:2, :] + u * cw[2:3, :]
    a_ref[...] = (zb * conv).astype(a_ref.dtype)
    last2 = u[tm - 2:tm, :]
    carry_ref[...] = last2
    nc_ref[...] = last2

    @pl.when(m == 0)
    def _():
        hs = hs_ref[...]
        zxs = _dot_hi(hs, wx_ref[...])
        zbs = _dot_hi(hs, wb_ref[...])
        zcs = _dot_hi(hs, wc_ref[...])
        us = zcs * zxs
        p0 = p0_ref[...]
        p1 = p1_ref[...]
        convs = p0 * cw[0:1, :] + p1 * cw[1:2, :] + us * cw[2:3, :]
        as_ref[...] = zbs * convs
        nc0_ref[...] = p1
        nc1_ref[...] = us


def conv_proj(h_p, h_s, w_in, conv_w, cache_conv2d, n_batch, tm=512, tn=256):
    mp = h_p.shape[0]
    ms = h_s.shape[0]
    tiles_per_seq = SEQ // tm
    nb = D_CONV // tn
    kern = functools.partial(_conv_proj_kernel, tm=tm, tiles_per_seq=tiles_per_seq)
    wspec = lambda off: pl.BlockSpec((D_MODEL, tn), lambda n, m: (0, n + off // tn))
    return pl.pallas_call(
        kern,
        out_shape=(jax.ShapeDtypeStruct((mp, D_CONV), BF16),
                   jax.ShapeDtypeStruct((n_batch, 2, D_CONV), F32),
                   jax.ShapeDtypeStruct((ms, D_CONV), F32),
                   jax.ShapeDtypeStruct((ms, D_CONV), F32),
                   jax.ShapeDtypeStruct((ms, D_CONV), F32)),
        grid=(nb, mp // tm),
        in_specs=[pl.BlockSpec((tm, D_MODEL), lambda n, m: (m, 0)),
                  pl.BlockSpec((ms, D_MODEL), lambda n, m: (0, 0)),
                  wspec(OFF_CX), wspec(OFF_CB), wspec(OFF_CC),
                  pl.BlockSpec((3, tn), lambda n, m: (0, n)),
                  pl.BlockSpec((ms, tn), lambda n, m: (0, n)),
                  pl.BlockSpec((ms, tn), lambda n, m: (0, n + nb))],
        out_specs=(pl.BlockSpec((tm, tn), lambda n, m: (m, n)),
                   pl.BlockSpec((None, 2, tn), lambda n, m: (m // tiles_per_seq, 0, n)),
                   pl.BlockSpec((ms, tn), lambda n, m: (0, n)),
                   pl.BlockSpec((ms, tn), lambda n, m: (0, n)),
                   pl.BlockSpec((ms, tn), lambda n, m: (0, n))),
        scratch_shapes=[pltpu.VMEM((2, tn), F32)],
        compiler_params=_params(("arbitrary", "arbitrary"), 56),
        name="conv_proj",
    )(h_p, h_s, w_in, w_in, w_in, conv_w, cache_conv2d, cache_conv2d)


def _head_norm_rope(z, gain, c, sa, sb, o_ref, tn):
    for hh in range(tn // HEAD_DIM):
        cs = slice(hh * HEAD_DIM, (hh + 1) * HEAD_DIM)
        y = _rms(z[:, cs], gain)
        y = y * c + pltpu.roll(y, HEAD_DIM - ROT_DIM // 2, 1) * sa + pltpu.roll(y, ROT_DIM // 2, 1) * sb
        o_ref[:, cs] = y.astype(o_ref.dtype)


def _qkv_proj_kernel(h_ref, hs_ref, wq_ref, wk_ref, wv_ref, qn_ref, kn_ref,
                     c_ref, sa_ref, sb_ref, cs_ref, sas_ref, sbs_ref,
                     q_ref, k_ref, v_ref, qs_ref, ks_ref, vs_ref, *, tn):
    m = pl.program_id(1)
    h = h_ref[...]
    c, sa, sb = c_ref[...], sa_ref[...], sb_ref[...]
    _head_norm_rope(_dot(h, wq_ref[...].astype(BF16)), qn_ref[...], c, sa, sb, q_ref, tn)
    _head_norm_rope(_dot(h, wk_ref[...].astype(BF16)), kn_ref[...], c, sa, sb, k_ref, tn)
    v_ref[...] = _dot(h, wv_ref[...].astype(BF16))

    @pl.when(m == 0)
    def _():
        hs = hs_ref[...]
        c2, sa2, sb2 = cs_ref[...], sas_ref[...], sbs_ref[...]
        _head_norm_rope(_dot_hi(hs, wq_ref[...]), qn_ref[...], c2, sa2, sb2, qs_ref, tn)
        _head_norm_rope(_dot_hi(hs, wk_ref[...]), kn_ref[...], c2, sa2, sb2, ks_ref, tn)
        vs_ref[...] = _dot_hi(hs, wv_ref[...])


def qkv_proj(h_p, h_s, w_in, q_norm, k_norm, rope_p, rope_s, tm=512, tn=256):
    mp = h_p.shape[0]
    ms = h_s.shape[0]
    tiles_per_seq = SEQ // tm
    kern = functools.partial(_qkv_proj_kernel, tn=tn)
    wspec = lambda off: pl.BlockSpec((D_MODEL, tn), lambda n, m: (0, n + off // tn))
    small = pl.BlockSpec((1, HEAD_DIM), lambda n, m: (0, 0))
    tab = pl.BlockSpec((tm, HEAD_DIM), lambda n, m: (m % tiles_per_seq, 0))
    out_p = pl.BlockSpec((tm, tn), lambda n, m: (m, n))
    out_s = pl.BlockSpec((ms, tn), lambda n, m: (0, n))
    return pl.pallas_call(
        kern,
        out_shape=(jax.ShapeDtypeStruct((mp, D_ATTN), F32),) * 3
        + (jax.ShapeDtypeStruct((ms, D_ATTN), F32),) * 3,
        grid=(D_ATTN // tn, mp // tm),
        in_specs=[pl.BlockSpec((tm, D_MODEL), lambda n, m: (m, 0)),
                  pl.BlockSpec((ms, D_MODEL), lambda n, m: (0, 0)),
                  wspec(OFF_Q), wspec(OFF_K), wspec(OFF_V),
                  small, small, tab, tab, tab, small, small, small],
        out_specs=(out_p, out_p, out_p, out_s, out_s, out_s),
        compiler_params=_params(("arbitrary", "arbitrary"), 56),
        name="qkv_proj",
    )(h_p, h_s, w_in, w_in, w_in, q_norm.reshape(1, HEAD_DIM), k_norm.reshape(1, HEAD_DIM),
      *rope_p, *rope_s)


def _mq_proj_kernel(h_ref, hs_ref, w_ref, g_ref, o_ref, os_ref):
    m = pl.program_id(1)
    o_ref[...] = _rms(_dot(h_ref[...], w_ref[...].astype(BF16)), g_ref[...]).astype(o_ref.dtype)

    @pl.when(m == 0)
    def _():
        os_ref[...] = _rms(_dot_hi(hs_ref[...], w_ref[...]), g_ref[...])


def mq_proj(h_p, h_s, w_in, mq_norm, tm=1024):
    mp = h_p.shape[0]
    ms = h_s.shape[0]
    tn = MEM_HEAD_DIM
    return pl.pallas_call(
        _mq_proj_kernel,
        out_shape=(jax.ShapeDtypeStruct((mp, D_MEM), BF16),
                   jax.ShapeDtypeStruct((ms, D_MEM), F32)),
        grid=(D_MEM // tn, mp // tm),
        in_specs=[pl.BlockSpec((tm, D_MODEL), lambda n, m: (m, 0)),
                  pl.BlockSpec((ms, D_MODEL), lambda n, m: (0, 0)),
                  pl.BlockSpec((D_MODEL, tn), lambda n, m: (0, n + OFF_MQ // tn)),
                  pl.BlockSpec((1, tn), lambda n, m: (0, 0))],
        out_specs=(pl.BlockSpec((tm, tn), lambda n, m: (m, n)),
                   pl.BlockSpec((ms, tn), lambda n, m: (0, n))),
        compiler_params=_params(("arbitrary", "arbitrary"), 48),
        name="mq_proj",
    )(h_p, h_s, w_in, mq_norm.reshape(1, tn))


def _mem_kv_kernel(h_ref, w_ref, g_ref, o_ref, *, head_norm):
    z = _dot(h_ref[...], w_ref[...].astype(BF16))
    if head_norm:
        z = _rms(z, g_ref[...])
    o_ref[...] = z


def mem_kv_proj(hm, w_mem_kv, mk_norm, col_off, head_norm):
    m = hm.shape[0]
    tn = MEM_HEAD_DIM
    kern = functools.partial(_mem_kv_kernel, head_norm=head_norm)
    return pl.pallas_call(
        kern,
        out_shape=jax.ShapeDtypeStruct((m, D_MEM), F32),
        grid=(D_MEM // tn,),
        in_specs=[pl.BlockSpec((m, D_MODEL), lambda n: (0, 0)),
                  pl.BlockSpec((D_MODEL, tn), lambda n: (0, n + col_off // tn)),
                  pl.BlockSpec((1, tn), lambda n: (0, 0))],
        out_specs=pl.BlockSpec((m, tn), lambda n: (0, n)),
        compiler_params=_params(("arbitrary",), 48),
        name="mem_kv_proj",
    )(hm, w_mem_kv, mk_norm.reshape(1, tn))


def _dil_attn_kernel(q_ref, k_ref, v_ref, o_ref, l_ref, *, seq_len):
    nwin = 128
    nb = seq_len // nwin
    scale = HEAD_DIM ** -0.5
    ri = lax.broadcasted_iota(jnp.int32, (nwin, 2 * nwin), 0)
    ci = lax.broadcasted_iota(jnp.int32, (nwin, 2 * nwin), 1)
    band = (ci >= ri) & (ci <= ri + nwin)
    causal = (lax.broadcasted_iota(jnp.int32, (nwin, nwin), 1)
              <= lax.broadcasted_iota(jnp.int32, (nwin, nwin), 0))

    def attend(hh, row0, key0, nkeys, mask):
        cs = slice(hh * HEAD_DIM, (hh + 1) * HEAD_DIM)
        qb = q_ref[pl.ds(row0, nwin), cs].astype(BF16)
        kb = k_ref[pl.ds(key0, nkeys), cs].astype(BF16)
        vb = v_ref[pl.ds(key0, nkeys), cs].astype(BF16)
        s = _dot_nt(qb, kb) * scale
        s = jnp.where(mask, s, NEG_INF)
        mx = jnp.max(s, axis=-1, keepdims=True)
        p = jnp.exp(s - mx)
        den = jnp.sum(p, axis=-1, keepdims=True)
        o_ref[pl.ds(row0, nwin), cs] = _dot(p.astype(BF16), vb) / den
        l_ref[pl.ds(row0, nwin), cs] = jnp.broadcast_to(mx + jnp.log(den), (nwin, HEAD_DIM))

    for hh in range(HEADS_PER_GROUP):
        attend(hh, 0, 0, nwin, causal)
        if nb > 1:
            def body(b, carry, hh=hh):
                row0 = pl.multiple_of(b * nwin, nwin)
                key0 = pl.multiple_of(b * nwin - nwin, nwin)
                attend(hh, row0, key0, 2 * nwin, band)
                return carry
            lax.fori_loop(1, nb, body, 0)


def dilated_attention_prompt(q, k, v, g, n_batch):
    d = DILATIONS[g]
    sl = SEQ // d
    view = lambda t: t.reshape(n_batch, sl, d * D_ATTN)
    in_spec = pl.BlockSpec((None, sl, D_GROUP), lambda b, r: (b, 0, r * 3 + g))
    out_spec = pl.BlockSpec((None, sl, D_GROUP), lambda b, r: (b, 0, r))
    kern = functools.partial(_dil_attn_kernel, seq_len=sl)
    o, lse = pl.pallas_call(
        kern,
        out_shape=(jax.ShapeDtypeStruct((n_batch, sl, d * D_GROUP), F32),) * 2,
        grid=(n_batch, d),
        in_specs=[in_spec, in_spec, in_spec],
        out_specs=(out_spec, out_spec),
        compiler_params=_params(("arbitrary", "arbitrary"), 56),
        name=f"dilated_attention_g{g}",
    )(view(q), view(k), view(v))
    return o.reshape(n_batch * SEQ, D_GROUP), lse.reshape(n_batch * SEQ, D_GROUP)


def _lse_combine_kernel(o0, o1, o2, l0, l1, l2, att_ref):
    a, b, c = l0[...], l1[...], l2[...]
    mx = jnp.maximum(jnp.maximum(a, b), c)
    ea, eb, ec = jnp.exp(a - mx), jnp.exp(b - mx), jnp.exp(c - mx)
    den = ea + eb + ec
    att_ref[:, 0:D_GROUP] = (o0[...] * (ea / den)).astype(att_ref.dtype)
    att_ref[:, D_GROUP:2 * D_GROUP] = (o1[...] * (eb / den)).astype(att_ref.dtype)
    att_ref[:, 2 * D_GROUP:] = (o2[...] * (ec / den)).astype(att_ref.dtype)


def lse_combine(outs, lses, tm=1024):
    m = outs[0].shape[0]
    spec = pl.BlockSpec((tm, D_GROUP), lambda i: (i, 0))
    return pl.pallas_call(
        _lse_combine_kernel,
        out_shape=jax.ShapeDtypeStruct((m, D_ATTN), BF16),
        grid=(m // tm,),
        in_specs=[spec] * 6,
        out_specs=pl.BlockSpec((tm, D_ATTN), lambda i: (i, 0)),
        compiler_params=_params(("arbitrary",), 48),
        name="lse_combine",
    )(*outs, *lses)


def _sample_attn_kernel(q_ref, kn_ref, vn_ref, k0, v0, k1, v1, k2, v2, att_ref):
    scale = HEAD_DIM ** -0.5
    caches = ((k0, v0), (k1, v1), (k2, v2))
    for hh in range(HEADS_PER_GROUP):
        outs, lses = [], []
        for g in range(3):
            kc, vc = caches[g]
            cs = slice(g * D_GROUP + hh * HEAD_DIM, g * D_GROUP + (hh + 1) * HEAD_DIM)
            cc = slice(hh * HEAD_DIM, (hh + 1) * HEAD_DIM)
            q = q_ref[:, cs]
            s_old = jnp.sum(kc[:, cc] * q, axis=-1, keepdims=True) * scale
            s_new = jnp.sum(kn_ref[:, cs] * q, axis=-1, keepdims=True) * scale
            mx = jnp.maximum(jnp.max(s_old, axis=0, keepdims=True), s_new)
            p_old = jnp.exp(s_old - mx)
            p_new = jnp.exp(s_new - mx)
            den = jnp.sum(p_old, axis=0, keepdims=True) + p_new
            o = (jnp.sum(p_old * vc[:, cc], axis=0, keepdims=True) + p_new * vn_ref[:, cs]) / den
            outs.append(o)
            lses.append(mx + jnp.log(den))
        mx = jnp.maximum(jnp.maximum(lses[0], lses[1]), lses[2])
        es = [jnp.exp(l - mx) for l in lses]
        den = es[0] + es[1] + es[2]
        for g in range(3):
            cs = slice(g * D_GROUP + hh * HEAD_DIM, g * D_GROUP + (hh + 1) * HEAD_DIM)
            att_ref[:, cs] = outs[g] * (es[g] / den)


def dilated_attention_sample(q_s, k_s, v_s, caches):
    n = q_s.shape[0]
    row = pl.BlockSpec((None, 1, D_ATTN), lambda b: (b, 0, 0))
    args, specs = [], []
    for g, (kc, vc) in enumerate(caches):
        d = DILATIONS[g]
        length = kc.shape[1]
        assert length == WINDOWS[g] and length // d == 128
        for t in (kc, vc):
            args.append(t.reshape(n, length // d, d * D_GROUP))
            specs.append(pl.BlockSpec((None, length // d, D_GROUP), lambda b: (b, 0, 0)))
    r3 = lambda t: t.reshape(n, 1, D_ATTN)
    att = pl.pallas_call(
        _sample_attn_kernel,
        out_shape=jax.ShapeDtypeStruct((n, 1, D_ATTN), F32),
        grid=(n,),
        in_specs=[row, row, row] + specs,
        out_specs=row,
        compiler_params=_params(("arbitrary",), 32),
        name="dilated_attention_sample",
    )(r3(q_s), r3(k_s), r3(v_s), *args)
    return att.reshape(n, D_ATTN)


def _mem_attn_kernel(q_ref, k_ref, v_ref, o_ref):
    scale = MEM_HEAD_DIM ** -0.5
    for hh in range(MEM_HEADS):
        cs = slice(hh * MEM_HEAD_DIM, (hh + 1) * MEM_HEAD_DIM)
        s = _dot_nt(q_ref[:, cs], k_ref[:, cs].astype(BF16)) * scale
        mx = jnp.max(s, axis=-1, keepdims=True)
        p = jnp.exp(s - mx)
        p = p / jnp.sum(p, axis=-1, keepdims=True)
        o_ref[:, cs] = _dot(p.astype(BF16), v_ref[:, cs].astype(BF16)).astype(o_ref.dtype)


def memory_attention_prompt(mq, mem_k, mem_v, n_batch, tq=512):
    tiles = SEQ // tq
    kv = pl.BlockSpec((MEM_LEN, D_MEM), lambda b, i: (b, 0))
    qo = pl.BlockSpec((tq, D_MEM), lambda b, i: (b * tiles + i, 0))
    return pl.pallas_call(
        _mem_attn_kernel,
        out_shape=jax.ShapeDtypeStruct(mq.shape, BF16),
        grid=(n_batch, tiles),
        in_specs=[qo, kv, kv],
        out_specs=qo,
        compiler_params=_params(("arbitrary", "arbitrary"), 32),
        name="memory_attention_prompt",
    )(mq, mem_k, mem_v)


def _mem_attn_sample_kernel(q_ref, k_ref, v_ref, o_ref):
    scale = MEM_HEAD_DIM ** -0.5
    for hh in range(MEM_HEADS):
        cs = slice(hh * MEM_HEAD_DIM, (hh + 1) * MEM_HEAD_DIM)
        s = jnp.sum(k_ref[:, cs] * q_ref[:, cs], axis=-1, keepdims=True) * scale
        mx = jnp.max(s, axis=0, keepdims=True)
        p = jnp.exp(s - mx)
        p = p / jnp.sum(p, axis=0, keepdims=True)
        o_ref[:, cs] = jnp.sum(p * v_ref[:, cs], axis=0, keepdims=True)


def memory_attention_sample(mq_s, cache_k, cache_v):
    n = mq_s.shape[0]
    row = pl.BlockSpec((None, 1, D_MEM), lambda b: (b, 0, 0))
    kv = pl.BlockSpec((None, MEM_LEN, D_MEM), lambda b: (b, 0, 0))
    o = pl.pallas_call(
        _mem_attn_sample_kernel,
        out_shape=jax.ShapeDtypeStruct((n, 1, D_MEM), F32),
        grid=(n,),
        in_specs=[row, kv, kv],
        out_specs=row,
        compiler_params=_params(("arbitrary",), 32),
        name="memory_attention_sample",
    )(mq_s.reshape(n, 1, D_MEM), cache_k.reshape(n, MEM_LEN, D_MEM), cache_v.reshape(n, MEM_LEN, D_MEM))
    return o.reshape(n, D_MEM)


def _merge_body(h, a, t, o, wga, wgb, wgc, bga, bgb, bgc, wco, wao, wmo, dot):
    mix = jax.nn.sigmoid(dot(h, wga) + bga) * dot(a, wco)
    mix = mix + jax.nn.sigmoid(dot(h, wgb) + bgb) * dot(t, wao)
    return mix + jax.nn.sigmoid(dot(h, wgc) + bgc) * dot(o, wmo)


def _merge_kernel(h_ref, a_ref, t_ref, o_ref, wga, wgb, wgc, bga, bgb, bgc, wco, wao, wmo, mix_ref, *, hi):
    if hi:
        dot = _dot_hi
        cast = lambda w: w[...]
    else:
        dot = _dot
        cast = lambda w: w[...].astype(BF16)
    mix = _merge_body(h_ref[...], a_ref[...], t_ref[...], o_ref[...], cast(wga), cast(wgb), cast(wgc),
                      bga[...], bgb[...], bgc[...], cast(wco), cast(wao), cast(wmo), dot)
    mix_ref[...] = mix.astype(mix_ref.dtype)


def gated_merge(h, a, t, o, w_gate, b_gate, w_conv_out, w_attn_out, w_mem_out, gate_col_off,
                hi, out_dtype, tm, tn=256):
    m = h.shape[0]
    kern = functools.partial(_merge_kernel, hi=hi)
    lhs = lambda width: pl.BlockSpec((tm, width), lambda n, i: (i, 0))
    gspec = lambda j: pl.BlockSpec((D_MODEL, tn), lambda n, i: (0, n + (gate_col_off + j * D_MODEL) // tn))
    bspec = lambda j: pl.BlockSpec((1, tn), lambda n, i: (0, n + j * D_MODEL // tn))
    wspec = lambda width: pl.BlockSpec((width, tn), lambda n, i: (0, n))
    return pl.pallas_call(
        kern,
        out_shape=jax.ShapeDtypeStruct((m, D_MODEL), out_dtype),
        grid=(D_MODEL // tn, m // tm),
        in_specs=[lhs(D_MODEL), lhs(D_CONV), lhs(D_ATTN), lhs(D_MEM),
                  gspec(0), gspec(1), gspec(2), bspec(0), bspec(1), bspec(2),
                  wspec(D_CONV), wspec(D_ATTN), wspec(D_MEM)],
        out_specs=pl.BlockSpec((tm, tn), lambda n, i: (i, n)),
        compiler_params=_params(("arbitrary", "arbitrary"), 56),
        name="gated_merge_hi" if hi else "gated_merge",
    )(h, a, t, o, w_gate, w_gate, w_gate, b_gate, b_gate, b_gate, w_conv_out, w_attn_out, w_mem_out)


def _out_proj_kernel(mix_ref, mixs_ref, w_ref, x_ref, xs_ref, o_ref, os_ref):
    m = pl.program_id(1)
    o_ref[...] = x_ref[...] + _dot(mix_ref[...], w_ref[...].astype(BF16))

    @pl.when(m == 0)
    def _():
        os_ref[...] = xs_ref[...] + _dot_hi(mixs_ref[...], w_ref[...])


def out_proj(mix_p, mix_s, w_o, x_p, x_s, tm=1024, tn=512):
    mp = mix_p.shape[0]
    ms = mix_s.shape[0]
    return pl.pallas_call(
        _out_proj_kernel,
        out_shape=(jax.ShapeDtypeStruct((mp, D_MODEL), F32),
                   jax.ShapeDtypeStruct((ms, D_MODEL), F32)),
        grid=(D_MODEL // tn, mp // tm),
        in_specs=[pl.BlockSpec((tm, D_MODEL), lambda n, m: (m, 0)),
                  pl.BlockSpec((ms, D_MODEL), lambda n, m: (0, 0)),
                  pl.BlockSpec((D_MODEL, tn), lambda n, m: (0, n)),
                  pl.BlockSpec((tm, tn), lambda n, m: (m, n)),
                  pl.BlockSpec((ms, tn), lambda n, m: (0, n))],
        out_specs=(pl.BlockSpec((tm, tn), lambda n, m: (m, n)),
                   pl.BlockSpec((ms, tn), lambda n, m: (0, n))),
        compiler_params=_params(("arbitrary", "arbitrary"), 56),
        name="out_proj",
    )(mix_p, mix_s, w_o, x_p, x_s)


def _norm_router_kernel(x_ref, g_ref, w_ref, h_ref, l_ref, *, hi):
    h = _rms(x_ref[...], g_ref[...])
    h_ref[...] = h.astype(h_ref.dtype)
    if hi:
        l_ref[...] = _dot_hi(h, w_ref[...])
    else:
        l_ref[...] = _dot(h.astype(BF16), w_ref[...].astype(BF16))


def norm_router(x, gain, w_router, hi, tm):
    m = x.shape[0]
    kern = functools.partial(_norm_router_kernel, hi=hi)
    return pl.pallas_call(
        kern,
        out_shape=(jax.ShapeDtypeStruct((m, D_MODEL), BF16),
                   jax.ShapeDtypeStruct((m, N_ROUTER_PAD), F32)),
        grid=(m // tm,),
        in_specs=[pl.BlockSpec((tm, D_MODEL), lambda i: (i, 0)),
                  pl.BlockSpec((1, D_MODEL), lambda i: (0, 0)),
                  pl.BlockSpec((D_MODEL, N_ROUTER_PAD), lambda i: (0, 0))],
        out_specs=(pl.BlockSpec((tm, D_MODEL), lambda i: (i, 0)),
                   pl.BlockSpec((tm, N_ROUTER_PAD), lambda i: (i, 0))),
        compiler_params=_params(("arbitrary",), 48),
        name="norm_router_hi" if hi else "norm_router",
    )(x, gain.reshape(1, D_MODEL), w_router)


def _moe_up_kernel(be_ref, nu_ref, x_ref, wg_ref, wu_ref, a_ref):
    i = pl.program_id(1)

    @pl.when(i < nu_ref[0])
    def _():
        x = x_ref[...]
        gate = _dot(x, wg_ref[...].astype(BF16))
        up = _dot(x, wu_ref[...].astype(BF16))
        a_ref[...] = (jax.nn.silu(gate) * up).astype(a_ref.dtype)

    @pl.when(i >= nu_ref[0])
    def _():
        a_ref[...] = jnp.zeros_like(a_ref)


def moe_up(block_e, n_used, xbuf, w_gate, w_up, tj=512):
    n_blk = xbuf.shape[0] // MOE_BLK
    wspec = pl.BlockSpec((None, D_MODEL, tj), lambda j, i, be, nu: (be[i], 0, j))
    return pl.pallas_call(
        _moe_up_kernel,
        out_shape=jax.ShapeDtypeStruct((n_blk * MOE_BLK, D_EXPERT), BF16),
        grid_spec=pltpu.PrefetchScalarGridSpec(
            num_scalar_prefetch=2,
            grid=(D_EXPERT // tj, n_blk),
            in_specs=[pl.BlockSpec((MOE_BLK, D_MODEL), lambda j, i, be, nu: (i, 0)), wspec, wspec],
            out_specs=pl.BlockSpec((MOE_BLK, tj), lambda j, i, be, nu: (i, j))),
        compiler_params=_params(("arbitrary", "arbitrary"), 56),
        name="moe_up",
    )(block_e, n_used, xbuf, w_gate, w_up)


def _moe_down_kernel(be_ref, nu_ref, a_ref, wd_ref, rw_ref, y_ref):
    i = pl.program_id(1)

    @pl.when(i < nu_ref[0])
    def _():
        y_ref[...] = _dot(a_ref[...], wd_ref[...].astype(BF16)) * rw_ref[...]

    @pl.when(i >= nu_ref[0])
    def _():
        y_ref[...] = jnp.zeros_like(y_ref)


def moe_down(block_e, n_used, abuf, w_down, row_w, tn=1024):
    n_blk = abuf.shape[0] // MOE_BLK
    return pl.pallas_call(
        _moe_down_kernel,
        out_shape=jax.ShapeDtypeStruct((n_blk * MOE_BLK, D_MODEL), F32),
        grid_spec=pltpu.PrefetchScalarGridSpec(
            num_scalar_prefetch=2,
            grid=(D_MODEL // tn, n_blk),
            in_specs=[pl.BlockSpec((MOE_BLK, D_EXPERT), lambda n, i, be, nu: (i, 0)),
                      pl.BlockSpec((None, D_EXPERT, tn), lambda n, i, be, nu: (be[i], 0, n)),
                      pl.BlockSpec((MOE_BLK, 1), lambda n, i, be, nu: (i, 0))],
            out_specs=pl.BlockSpec((MOE_BLK, tn), lambda n, i, be, nu: (i, n))),
        compiler_params=_params(("arbitrary", "arbitrary"), 48),
        name="moe_down",
    )(block_e, n_used, abuf, w_down, row_w)


def _route(logits, b_rg, b_re):
    n_tok = logits.shape[0]
    g_logits = logits[:, :N_EXPERT_GROUPS] + b_rg.astype(F32)
    g_idx = jnp.argmax(g_logits, axis=-1)
    g_gate = jnp.max(jax.nn.softmax(g_logits, axis=-1), axis=-1, keepdims=True)
    e_logits = (logits[:, N_EXPERT_GROUPS:N_EXPERT_GROUPS + N_EXPERTS] + b_re.astype(F32))
    e_logits = e_logits.reshape(n_tok, N_EXPERT_GROUPS, EXPERTS_PER_GROUP)[jnp.arange(n_tok), g_idx]
    top_v, top_i = lax.top_k(e_logits, TOP_K)
    gate = jax.nn.softmax(top_v, axis=-1) * g_gate
    expert = g_idx[:, None] * EXPERTS_PER_GROUP + top_i
    return expert.astype(jnp.int32), gate


def hier_moe(h_all, logits, b_rg, b_re, w_gate, w_up, w_down):
    n_tok = h_all.shape[0]
    expert, gate = _route(logits, b_rg, b_re)
    n_asg = n_tok * TOP_K
    n_blk = -(-(n_asg + N_EXPERTS * (MOE_BLK - 1)) // MOE_BLK)
    n_rows = n_blk * MOE_BLK
    e_flat = expert.reshape(-1)
    w_flat = gate.reshape(-1)
    order = jnp.argsort(e_flat)
    e_s = e_flat[order]
    tok_s = (order // TOP_K).astype(jnp.int32)
    w_s = w_flat[order]
    counts = jnp.bincount(e_flat, length=N_EXPERTS)
    padded = ((counts + MOE_BLK - 1) // MOE_BLK) * MOE_BLK
    start = jnp.cumsum(counts) - counts
    pend = jnp.cumsum(padded)
    pstart = pend - padded
    dest = (pstart[e_s] + jnp.arange(n_asg) - start[e_s]).astype(jnp.int32)
    block_e = jnp.clip(jnp.searchsorted(pend, jnp.arange(n_blk) * MOE_BLK, side='right'),
                       0, N_EXPERTS - 1).astype(jnp.int32)
    n_used = (pend[-1] // MOE_BLK).astype(jnp.int32).reshape(1)
    row_src = jnp.full((n_rows,), n_tok, jnp.int32).at[dest].set(tok_s)
    row_w = jnp.zeros((n_rows,), F32).at[dest].set(w_s).reshape(n_rows, 1)
    xbuf = jnp.take(h_all, row_src, axis=0, mode='fill', fill_value=0)
    abuf = moe_up(block_e, n_used, xbuf, w_gate, w_up)
    ybuf = moe_down(block_e, n_used, abuf, w_down, row_w)
    dest_of_asg = jnp.zeros((n_asg,), jnp.int32).at[order].set(dest).reshape(n_tok, TOP_K)
    return ybuf[dest_of_asg[:, 0]] + ybuf[dest_of_asg[:, 1]]


def _rope_tables(pos):
    half = ROT_DIM // 2
    inv_freq = ROPE_THETA ** (-jnp.arange(half, dtype=F32) / half)
    ang = pos.astype(F32)[:, None] * inv_freq[None, :]
    cos, sin = jnp.cos(ang), jnp.sin(ang)
    n = pos.shape[0]
    c = jnp.concatenate([cos, cos, jnp.ones((n, HEAD_DIM - ROT_DIM), F32)], axis=1)
    sa = jnp.concatenate([-sin, jnp.zeros((n, HEAD_DIM - half), F32)], axis=1)
    sb = jnp.concatenate([jnp.zeros((n, half), F32), sin, jnp.zeros((n, HEAD_DIM - ROT_DIM), F32)], axis=1)
    return c, sa, sb


def kernel(x_prompt, x_sample, mem_prompt, cache_conv, cache_k_w128, cache_v_w128, cache_k_w512, cache_v_w512, cache_k_w2048, cache_v_w2048, cache_mem_k, cache_mem_v, norm1, w_in, b_gate, conv_w, q_norm, k_norm, mem_norm, w_mem_kv, mq_norm, mk_norm, w_conv_out, w_attn_out, w_mem_out, w_o, norm2, w_router_group, b_router_group, w_router_expert, b_router_expert, w_exp_gate, w_exp_up, w_exp_down):
    nb, seq, _ = x_prompt.shape
    ns = x_sample.shape[0]
    assert seq == SEQ and x_sample.shape[1] == 1 and norm1.shape[0] == 1
    mp = nb * seq
    xp = x_prompt.reshape(mp, D_MODEL)
    xs = x_sample.reshape(ns, D_MODEL)
    w_in0 = w_in[0]

    h_p = rmsnorm_rows(xp, norm1[0], BF16, 512)
    h_s = rmsnorm_rows(xs, norm1[0], F32, ns)

    a_p, nconv_p, a_s, nconv_s0, nconv_s1 = conv_proj(
        h_p, h_s, w_in0, conv_w[0], cache_conv[0].reshape(ns, 2 * D_CONV), nb)

    rope_p = _rope_tables(jnp.arange(SEQ))
    rope_s = _rope_tables(PAST_LEN + jnp.arange(1))
    q_p, k_p, v_p, q_s, k_s, v_s = qkv_proj(h_p, h_s, w_in0, q_norm[0], k_norm[0], rope_p, rope_s)
    outs, lses = [], []
    for g in range(3):
        o, lse = dilated_attention_prompt(q_p, k_p, v_p, g, nb)
        outs.append(o)
        lses.append(lse)
    att_p = lse_combine(outs, lses)
    caches = ((cache_k_w128[0], cache_v_w128[0]), (cache_k_w512[0], cache_v_w512[0]),
              (cache_k_w2048[0], cache_v_w2048[0]))
    att_s = dilated_attention_sample(q_s, k_s, v_s, caches)

    hm = rmsnorm_rows(mem_prompt.reshape(nb * MEM_LEN, D_MODEL), mem_norm[0], BF16, 512)
    mem_k = mem_kv_proj(hm, w_mem_kv[0], mk_norm[0], 0, True)
    mem_v = mem_kv_proj(hm, w_mem_kv[0], mk_norm[0], D_MEM, False)
    mq_p, mq_s = mq_proj(h_p, h_s, w_in0, mq_norm[0])
    om_p = memory_attention_prompt(mq_p, mem_k, mem_v, nb)
    om_s = memory_attention_sample(mq_s, cache_mem_k[0], cache_mem_v[0])

    b_gate2 = b_gate[0].reshape(1, 3 * D_MODEL)
    w_gate_bf = w_in0[:, OFF_G:].astype(BF16)
    mix_p = gated_merge(h_p, a_p, att_p, om_p, w_gate_bf, b_gate2, w_conv_out[0].astype(BF16),
                        w_attn_out[0].astype(BF16), w_mem_out[0].astype(BF16), 0, False, BF16, 512)
    mix_s = gated_merge(h_s, a_s, att_s, om_s, w_in0, b_gate2, w_conv_out[0], w_attn_out[0], w_mem_out[0],
                        OFF_G, True, F32, ns)
    xmid_p, xmid_s = out_proj(mix_p, mix_s, w_o[0], xp, xs)

    w_router = jnp.concatenate(
        [w_router_group[0], w_router_expert[0],
         jnp.zeros((D_MODEL, N_ROUTER_PAD - N_EXPERT_GROUPS - N_EXPERTS), F32)], axis=1)
    h2_p, logit_p = norm_router(xmid_p, norm2[0], w_router, False, 512)
    h2_s, logit_s = norm_router(xmid_s, norm2[0], w_router, True, ns)
    y_all = hier_moe(jnp.concatenate([h2_p, h2_s], axis=0), jnp.concatenate([logit_p, logit_s], axis=0),
                     b_router_group[0], b_router_expert[0], w_exp_gate[0], w_exp_up[0], w_exp_down[0])
    y_prompt = (xmid_p + y_all[:mp]).reshape(nb, seq, D_MODEL)
    y_sample = (xmid_s + y_all[mp:]).reshape(ns, 1, D_MODEL)

    k4 = k_p.reshape(nb, seq, 3, HEADS_PER_GROUP, HEAD_DIM)
    v4 = v_p.reshape(nb, seq, 3, HEADS_PER_GROUP, HEAD_DIM)
    ks4 = k_s.reshape(ns, 1, 3, HEADS_PER_GROUP, HEAD_DIM)
    vs4 = v_s.reshape(ns, 1, 3, HEADS_PER_GROUP, HEAD_DIM)
    kv_prompt, kv_sample = [], []
    for g in range(3):
        keep = min(WINDOWS[g], seq)
        kv_prompt += [k4[None, :, seq - keep:, g], v4[None, :, seq - keep:, g]]
        kc, vc = caches[g]
        kv_sample += [jnp.concatenate([kc[:, 1:], ks4[:, :, g]], axis=1)[None],
                      jnp.concatenate([vc[:, 1:], vs4[:, :, g]], axis=1)[None]]
    new_conv_s = jnp.stack([nconv_s0, nconv_s1], axis=1)[None]
    mem_shape = (1, nb, MEM_LEN, MEM_HEADS, MEM_HEAD_DIM)
    return (y_prompt, y_sample, nconv_p[None], new_conv_s, *kv_prompt, *kv_sample,
            mem_k.reshape(mem_shape), mem_v.reshape(mem_shape))
```
